```python
import jax, jax.numpy as jnp
from jax import lax
import numpy as np

D_MODEL = 2048
BATCH = 4
SEQ = 4096
DEPTH = 2

CHUNK = 64
GMLP_WINDOW = 128
GMLP_HEAD_DIM = D_MODEL // 16
N_GMLP_HEADS = 8
D_GMLP = N_GMLP_HEADS * GMLP_HEAD_DIM
LRU_BLOCK_DIM = D_MODEL // 16
N_LRU_BLOCKS = 8
D_LRU = N_LRU_BLOCKS * LRU_BLOCK_DIM
D_MIX = D_GMLP + D_LRU
D_IN = 2 * D_GMLP + 2 * D_LRU
CONV_WIDTH = 4
LRU_C = 8.0
D_FF = 5632
EPS = 1e-6

kernel_name = "hymba_style_gmlp_rglru_macaron"


def rms_norm(x, g):
    xf = x.astype(jnp.float32)
    y = xf * lax.rsqrt(jnp.mean(xf * xf, axis=-1, keepdims=True) + EPS)
    return (y * g.astype(jnp.float32)).astype(x.dtype)


def swiglu(x, w_gate, w_up, w_down):
    return (jax.nn.silu(x @ w_gate) * (x @ w_up)) @ w_down


def gmlp_spatial_gate(u, v, v_gain, w_s, b_s):
    B, S, _ = u.shape
    v = rms_norm(v, v_gain)
    n_win = S // GMLP_WINDOW
    vh = v.reshape(B, n_win, GMLP_WINDOW, N_GMLP_HEADS, GMLP_HEAD_DIM)
    pos = jnp.arange(GMLP_WINDOW)
    mask = (pos[None, :] // CHUNK) <= (pos[:, None] // CHUNK)
    w = jnp.where(mask[None], w_s, jnp.zeros((), w_s.dtype))
    mixed = jnp.einsum('hqp,bnphd->bnqhd', w, vh) + b_s.T[:, :, None]
    return u * mixed.reshape(B, S, D_GMLP)


def causal_depthwise_conv(x, w, b):
    D = x.shape[-1]
    y = lax.conv_general_dilated(
        x, w[:, None, :].astype(x.dtype), window_strides=(1,),
        padding=[(CONV_WIDTH - 1, 0)],
        dimension_numbers=('NWC', 'WIO', 'NWC'),
        feature_group_count=D)
    return y + b


def rg_lru(x, w_a, b_a, w_x, b_x, lam):
    B, S, D = x.shape
    xf = x.astype(jnp.float32)
    xb = xf.reshape(B, S, N_LRU_BLOCKS, LRU_BLOCK_DIM)
    r = jax.nn.sigmoid(jnp.einsum('bshi,hij->bshj', xb, w_a.astype(jnp.float32)).reshape(B, S, D)
                       + b_a.astype(jnp.float32))
    i = jax.nn.sigmoid(jnp.einsum('bshi,hij->bshj', xb, w_x.astype(jnp.float32)).reshape(B, S, D)
                       + b_x.astype(jnp.float32))
    log_a = -LRU_C * r * jax.nn.softplus(-lam.astype(jnp.float32))
    a = jnp.exp(log_a)
    gated_x = jnp.sqrt(-jnp.expm1(2.0 * log_a)) * (i * xf)

    def combine(left, right):
        a_l, h_l = left
        a_r, h_r = right
        return a_l * a_r, a_r * h_l + h_r

    _, h = lax.associative_scan(combine, (a, gated_x), axis=1)
    return h.astype(x.dtype)


def hybrid_mixer(h, w_in, gmlp_v_gain, gmlp_w_s, gmlp_b_s, conv_w, conv_b,
                 lru_w_a, lru_b_a, lru_w_x, lru_b_x, lru_lambda, w_out):
    z = h @ w_in
    u, v, g, xr = jnp.split(z, [D_GMLP, 2 * D_GMLP, 2 * D_GMLP + D_LRU], axis=-1)
    y_a = gmlp_spatial_gate(jax.nn.gelu(u), jax.nn.gelu(v), gmlp_v_gain, gmlp_w_s, gmlp_b_s)
    xr = causal_depthwise_conv(xr, conv_w, conv_b)
    y_b = rg_lru(xr, lru_w_a, lru_b_a, lru_w_x, lru_b_x, lru_lambda) * jax.nn.gelu(g)
    return jnp.concatenate([y_a, y_b], axis=-1) @ w_out


def setup_inputs(seed: int = 0) -> dict:
    key = jax.random.key(seed)
    ks = jax.random.split(key, 26)
    L = DEPTH

    def nrm(k, shape, scale):
        return jax.random.normal(k, shape, jnp.float32) * scale

    def gain(k, shape):
        return 1.0 + 0.02 * jax.random.normal(k, shape, jnp.float32)

    a_pow_c = jax.random.uniform(ks[20], (L, D_LRU), jnp.float32, minval=0.9, maxval=0.999)
    a_base = a_pow_c ** (1.0 / LRU_C)
    lru_lambda = jnp.log(a_base) - jnp.log1p(-a_base)

    return {
        "x": nrm(ks[0], (BATCH, SEQ, D_MODEL), 1.0),
        "ffn1_norm": gain(ks[1], (L, D_MODEL)),
        "ffn1_w_gate": nrm(ks[2], (L, D_MODEL, D_FF), D_MODEL ** -0.5),
        "ffn1_w_up": nrm(ks[3], (L, D_MODEL, D_FF), D_MODEL ** -0.5),
        "ffn1_w_down": nrm(ks[4], (L, D_FF, D_MODEL), D_FF ** -0.5),
        "mix_norm": gain(ks[5], (L, D_MODEL)),
        "w_in": nrm(ks[6], (L, D_MODEL, D_IN), D_MODEL ** -0.5),
        "gmlp_v_gain": gain(ks[7], (L, D_GMLP)),
        "gmlp_w_s": nrm(ks[8], (L, N_GMLP_HEADS, GMLP_WINDOW, GMLP_WINDOW), GMLP_WINDOW ** -0.5),
        "gmlp_b_s": 1.0 + 0.01 * jax.random.normal(ks[9], (L, N_GMLP_HEADS, GMLP_WINDOW), jnp.float32),
        "conv_w": nrm(ks[10], (L, CONV_WIDTH, D_LRU), CONV_WIDTH ** -0.5),
        "conv_b": nrm(ks[11], (L, D_LRU), 0.01),
        "lru_w_a": nrm(ks[12], (L, N_LRU_BLOCKS, LRU_BLOCK_DIM, LRU_BLOCK_DIM), LRU_BLOCK_DIM ** -0.5),
        "lru_b_a": nrm(ks[13], (L, D_LRU), 0.01),
        "lru_w_x": nrm(ks[14], (L, N_LRU_BLOCKS, LRU_BLOCK_DIM, LRU_BLOCK_DIM), LRU_BLOCK_DIM ** -0.5),
        "lru_b_x": nrm(ks[15], (L, D_LRU), 0.01),
        "lru_lambda": lru_lambda,
        "w_out": nrm(ks[16], (L, D_MIX, D_MODEL), D_MIX ** -0.5),
        "ffn2_norm": gain(ks[17], (L, D_MODEL)),
        "ffn2_w_gate": nrm(ks[18], (L, D_MODEL, D_FF), D_MODEL ** -0.5),
        "ffn2_w_up": nrm(ks[19], (L, D_MODEL, D_FF), D_MODEL ** -0.5),
        "ffn2_w_down": nrm(ks[21], (L, D_FF, D_MODEL), D_FF ** -0.5),
        "final_norm": gain(ks[22], (D_MODEL,)),
    }


def reference(x, ffn1_norm, ffn1_w_gate, ffn1_w_up, ffn1_w_down, mix_norm, w_in,
              gmlp_v_gain, gmlp_w_s, gmlp_b_s, conv_w, conv_b, lru_w_a, lru_b_a,
              lru_w_x, lru_b_x, lru_lambda, w_out, ffn2_norm, ffn2_w_gate, ffn2_w_up,
              ffn2_w_down, final_norm):
    for l in range(DEPTH):
        x = x + 0.5 * swiglu(rms_norm(x, ffn1_norm[l]), ffn1_w_gate[l], ffn1_w_up[l], ffn1_w_down[l])
        x = x + hybrid_mixer(rms_norm(x, mix_norm[l]), w_in[l], gmlp_v_gain[l], gmlp_w_s[l],
                             gmlp_b_s[l], conv_w[l], conv_b[l], lru_w_a[l], lru_b_a[l],
                             lru_w_x[l], lru_b_x[l], lru_lambda[l], w_out[l])
        x = x + 0.5 * swiglu(rms_norm(x, ffn2_norm[l]), ffn2_w_gate[l], ffn2_w_up[l], ffn2_w_down[l])
    return rms_norm(x, final_norm)
```

```python
import functools
import math

import jax
import jax.numpy as jnp
from jax import lax
from jax.experimental import pallas as pl
from jax.experimental.pallas import tpu as pltpu

CHUNK = 64
GMLP_WINDOW = 128
HEAD_DIM = 128
N_HEADS = 8
D_BRANCH = N_HEADS * HEAD_DIM
CONV_WIDTH = 4
LRU_C = 8.0
EPS = 1e-6

V7X_SUBLANES = 8
V7X_VMEM_LIMIT_BYTES = 60000 * 1024

FFN_TOKEN_TILE = 512
FFN_HIDDEN_TILE = 512
MIXER_SEQ_TILE = 256
LRU_UNROLL = 4

_F32 = jnp.float32
_BF16 = jnp.bfloat16


def _rms_norm(x, gain):
    return x * lax.rsqrt(jnp.mean(x * x, axis=-1, keepdims=True) + EPS) * gain


def _gelu_tanh(x):
    c = math.sqrt(2.0 / math.pi)
    return x * (0.5 * (1.0 + jnp.tanh(c * (x + 0.044715 * (x * x * x)))))


def _dot(a, b):
    return jnp.dot(a, b, preferred_element_type=_F32)


def _ffn_kernel(x_ref, g_ref, wg_ref, wu_ref, wd_ref, fg_ref, o_ref, xn_ref, *, final_norm):
    f = pl.program_id(1)
    last = pl.num_programs(1) - 1

    @pl.when(f == 0)
    def _():
        xn_ref[...] = _rms_norm(x_ref[...], g_ref[...]).astype(_BF16)

    xn = xn_ref[...]
    gate = _dot(xn, wg_ref[...])
    up = _dot(xn, wu_ref[...])
    h = (gate * jax.nn.sigmoid(gate) * up).astype(_BF16)
    part = _dot(h, wd_ref[...])

    @pl.when(f == 0)
    def _():
        o_ref[...] = part

    @pl.when(f > 0)
    def _():
        o_ref[...] += part

    @pl.when(f == last)
    def _():
        y = x_ref[...] + 0.5 * o_ref[...]
        if final_norm:
            y = _rms_norm(y, fg_ref[...])
        o_ref[...] = y


def _ffn(x2d, norm_gain, w_gate, w_up, w_down, final_gain, *, final_norm):
    T, D = x2d.shape
    F = w_gate.shape[1]
    tm, tf = FFN_TOKEN_TILE, FFN_HIDDEN_TILE
    assert T % tm == 0 and F % tf == 0
    return pl.pallas_call(
        functools.partial(_ffn_kernel, final_norm=final_norm),
        grid=(T // tm, F // tf),
        in_specs=[
            pl.BlockSpec((tm, D), lambda i, f: (i, 0)),
            pl.BlockSpec((1, D), lambda i, f: (0, 0)),
            pl.BlockSpec((D, tf), lambda i, f: (0, f)),
            pl.BlockSpec((D, tf), lambda i, f: (0, f)),
            pl.BlockSpec((tf, D), lambda i, f: (f, 0)),
            pl.BlockSpec((1, D), lambda i, f: (0, 0)),
        ],
        out_specs=pl.BlockSpec((tm, D), lambda i, f: (i, 0)),
        out_shape=jax.ShapeDtypeStruct((T, D), _F32),
        scratch_shapes=[pltpu.VMEM((tm, D), _BF16)],
        compiler_params=pltpu.CompilerParams(
            dimension_semantics=("parallel", "arbitrary"),
            vmem_limit_bytes=V7X_VMEM_LIMIT_BYTES),
        name="ffn_final" if final_norm else "ffn",
    )(x2d, norm_gain.reshape(1, D), w_gate, w_up, w_down, final_gain.reshape(1, D))


def _mixer_kernel(x_ref, g_ref, win_ref, vgain_ref, ws_ref, bs_ref, cw_ref, cb_ref,
                  wax_ref, ba_ref, bx_ref, lam_ref, wout_ref, o_ref,
                  y_ref, xpad_ref, a_ref, h_ref, hstate_ref):
    s = pl.program_id(1)
    ts = x_ref.shape[0]
    n_win = ts // GMLP_WINDOW
    pad = V7X_SUBLANES

    x = x_ref[...]
    xn = _rms_norm(x, g_ref[...]).astype(_BF16)

    gu = _gelu_tanh(_dot(xn, win_ref[:, 0:D_BRANCH]))
    gv = _gelu_tanh(_dot(xn, win_ref[:, D_BRANCH:2 * D_BRANCH]))
    vn = _rms_norm(gv, vgain_ref[...]).astype(_BF16)
    pos_q = lax.broadcasted_iota(jnp.int32, (GMLP_WINDOW, GMLP_WINDOW), 0)
    pos_p = lax.broadcasted_iota(jnp.int32, (GMLP_WINDOW, GMLP_WINDOW), 1)
    causal = (pos_p // CHUNK) <= (pos_q // CHUNK)
    for h in range(N_HEADS):
        cols = slice(h * HEAD_DIM, (h + 1) * HEAD_DIM)
        w_h = jnp.where(causal, ws_ref[h], 0.0).astype(_BF16)
        rhs = jnp.concatenate(
            [vn[w * GMLP_WINDOW:(w + 1) * GMLP_WINDOW, cols] for w in range(n_win)], axis=1)
        mixed = _dot(w_h, rhs)
        for w in range(n_win):
            rows = slice(w * GMLP_WINDOW, (w + 1) * GMLP_WINDOW)
            m = mixed[:, w * HEAD_DIM:(w + 1) * HEAD_DIM] + bs_ref[:, cols]
            y_ref[rows, cols] = (gu[rows, cols] * m).astype(_BF16)

    @pl.when(s == 0)
    def _():
        xpad_ref[0:pad, :] = jnp.zeros((pad, D_BRANCH), _F32)
        hstate_ref[...] = jnp.zeros(hstate_ref.shape, _F32)

    @pl.when(s > 0)
    def _():
        xpad_ref[0:pad, :] = xpad_ref[ts:ts + pad, :]

    xpad_ref[pad:pad + ts, :] = _dot(xn, win_ref[:, 3 * D_BRANCH:4 * D_BRANCH])
    xc = cb_ref[...] + cw_ref[CONV_WIDTH - 1:CONV_WIDTH, :] * xpad_ref[pad:pad + ts, :]
    for k in range(CONV_WIDTH - 1):
        off = pad - (CONV_WIDTH - 1) + k
        xc = xc + cw_ref[k:k + 1, :] * xpad_ref[off:off + ts, :]

    xcb = xc.astype(_BF16)
    neg_lam = -lam_ref[...]
    softplus_neg_lam = jnp.maximum(neg_lam, 0.0) + jnp.log1p(jnp.exp(-jnp.abs(neg_lam)))
    for h in range(N_HEADS):
        cols = slice(h * HEAD_DIM, (h + 1) * HEAD_DIM)
        ri = _dot(xcb[:, cols], wax_ref[h])
        r = jax.nn.sigmoid(ri[:, 0:HEAD_DIM] + ba_ref[:, cols])
        i = jax.nn.sigmoid(ri[:, HEAD_DIM:2 * HEAD_DIM] + bx_ref[:, cols])
        log_a = (-LRU_C) * r * softplus_neg_lam[:, cols]
        a = jnp.exp(log_a)
        a_ref[:, cols] = a
        one_minus_a2 = -jnp.tanh(log_a) * (a * a + 1.0)
        h_ref[:, cols] = jnp.sqrt(one_minus_a2) * (i * xc[:, cols])

    row = lax.broadcasted_iota(jnp.int32, (V7X_SUBLANES, D_BRANCH), 0)

    def group(gidx, carry):
        r0 = pl.multiple_of(gidx * V7X_SUBLANES, V7X_SUBLANES)
        a = a_ref[pl.ds(r0, V7X_SUBLANES), :]
        hh = h_ref[pl.ds(r0, V7X_SUBLANES), :]
        for k in (1, 2, 4):
            a_prev = pltpu.roll(a, k, axis=0)
            h_prev = pltpu.roll(hh, k, axis=0)
            keep = row >= k
            hh = jnp.where(keep, a * h_prev + hh, hh)
            a = jnp.where(keep, a * a_prev, a)
        hh = hh + a * carry
        h_ref[pl.ds(r0, V7X_SUBLANES), :] = hh
        return jnp.broadcast_to(hh[V7X_SUBLANES - 1:V7X_SUBLANES, :], hh.shape)

    hstate_ref[...] = lax.fori_loop(0, ts // V7X_SUBLANES, group, hstate_ref[...],
                                    unroll=LRU_UNROLL)

    gg = _gelu_tanh(_dot(xn, win_ref[:, 2 * D_BRANCH:3 * D_BRANCH]))
    y_ref[:, D_BRANCH:2 * D_BRANCH] = (h_ref[...] * gg).astype(_BF16)

    o_ref[...] = x + _dot(y_ref[...], wout_ref[...])


def _mixer(x, norm_gain, w_in, v_gain, w_s, b_s_full, conv_w, conv_b, w_ax, b_a, b_x,
           lam, w_out):
    B, S, D = x.shape
    ts = MIXER_SEQ_TILE
    assert S % ts == 0 and ts % GMLP_WINDOW == 0

    def whole(arr):
        nd = arr.ndim
        return pl.BlockSpec(arr.shape, lambda b, s: (0,) * nd, pipeline_mode=pl.Buffered(1))

    operands = (norm_gain.reshape(1, D), w_in, v_gain.reshape(1, D_BRANCH), w_s, b_s_full,
                conv_w, conv_b.reshape(1, D_BRANCH), w_ax, b_a.reshape(1, D_BRANCH),
                b_x.reshape(1, D_BRANCH), lam.reshape(1, D_BRANCH), w_out)
    return pl.pallas_call(
        _mixer_kernel,
        grid=(B, S // ts),
        in_specs=[pl.BlockSpec((None, ts, D), lambda b, s: (b, s, 0))]
                 + [whole(a) for a in operands],
        out_specs=pl.BlockSpec((None, ts, D), lambda b, s: (b, s, 0)),
        out_shape=jax.ShapeDtypeStruct((B, S, D), _F32),
        scratch_shapes=[
            pltpu.VMEM((ts, 2 * D_BRANCH), _BF16),
            pltpu.VMEM((ts + V7X_SUBLANES, D_BRANCH), _F32),
            pltpu.VMEM((ts, D_BRANCH), _F32),
            pltpu.VMEM((ts, D_BRANCH), _F32),
            pltpu.VMEM((V7X_SUBLANES, D_BRANCH), _F32),
        ],
        compiler_params=pltpu.CompilerParams(
            dimension_semantics=("arbitrary", "arbitrary"),
            vmem_limit_bytes=V7X_VMEM_LIMIT_BYTES),
        name="mixer",
    )(x, *operands)


def kernel(x, ffn1_norm, ffn1_w_gate, ffn1_w_up, ffn1_w_down, mix_norm, w_in, gmlp_v_gain, gmlp_w_s, gmlp_b_s, conv_w, conv_b, lru_w_a, lru_b_a, lru_w_x, lru_b_x, lru_lambda, w_out, ffn2_norm, ffn2_w_gate, ffn2_w_up, ffn2_w_down, final_norm):
    B, S, D = x.shape
    depth = ffn1_norm.shape[0]
    bf = lambda w: w.astype(_BF16)
    w_ax = bf(jnp.concatenate([lru_w_a, lru_w_x], axis=-1))
    b_s_full = jnp.repeat(jnp.swapaxes(gmlp_b_s, 1, 2), HEAD_DIM, axis=2)
    for l in range(depth):
        x2d = _ffn(x.reshape(B * S, D), ffn1_norm[l], bf(ffn1_w_gate[l]), bf(ffn1_w_up[l]),
                   bf(ffn1_w_down[l]), final_norm, final_norm=False)
        x = _mixer(x2d.reshape(B, S, D), mix_norm[l], bf(w_in[l]), gmlp_v_gain[l], gmlp_w_s[l],
                   b_s_full[l], conv_w[l], conv_b[l], w_ax[l], lru_b_a[l], lru_b_x[l],
                   lru_lambda[l], bf(w_out[l]))
        x2d = _ffn(x.reshape(B * S, D), ffn2_norm[l], bf(ffn2_w_gate[l]), bf(ffn2_w_up[l]),
                   bf(ffn2_w_down[l]), final_norm, final_norm=(l == depth - 1))
        x = x2d.reshape(B, S, D)
    return x
```

```python
import functools
import math

import jax
import jax.numpy as jnp
from jax import lax
from jax.experimental import pallas as pl
from jax.experimental.pallas import tpu as pltpu

CHUNK = 64
GMLP_WINDOW = 128
HEAD_DIM = 128
N_HEADS = 8
D_BRANCH = N_HEADS * HEAD_DIM
CONV_WIDTH = 4
LRU_C = 8.0
EPS = 1e-6

V7X_SUBLANES = 8
V7X_VMEM_LIMIT_BYTES = 60000 * 1024

FFN_TOKEN_TILE = 512
FFN_HIDDEN_TILE = 512
MIXER_SEQ_TILE = 256

_F32 = jnp.float32
_BF16 = jnp.bfloat16


def _rms_norm(x, gain):
    return x * lax.rsqrt(jnp.mean(x * x, axis=-1, keepdims=True) + EPS) * gain


def _gelu_tanh(x):
    c = math.sqrt(2.0 / math.pi)
    return x * (0.5 * (1.0 + jnp.tanh(c * (x + 0.044715 * (x * x * x)))))


def _dot(a, b):
    return jnp.dot(a, b, preferred_element_type=_F32)


def _ffn_kernel(x_ref, g_ref, wg_ref, wu_ref, wd_ref, fg_ref, o_ref, xn_ref, *, final_norm):
    f = pl.program_id(1)
    last = pl.num_programs(1) - 1

    @pl.when(f == 0)
    def _():
        x = x_ref[...]
        xn_ref[...] = _rms_norm(x, g_ref[...]).astype(_BF16)
        o_ref[...] = x

    xn = xn_ref[...]
    gate = _dot(xn, wg_ref[...])
    up = _dot(xn, wu_ref[...])
    h = (0.5 * gate * jax.nn.sigmoid(gate) * up).astype(_BF16)
    o_ref[...] += _dot(h, wd_ref[...])

    if final_norm:
        @pl.when(f == last)
        def _():
            o_ref[...] = _rms_norm(o_ref[...], fg_ref[...])


def _ffn(x2d, norm_gain, w_gate, w_up, w_down, final_gain, *, final_norm):
    T, D = x2d.shape
    F = w_gate.shape[1]
    tm, tf = FFN_TOKEN_TILE, FFN_HIDDEN_TILE
    assert T % tm == 0 and F % tf == 0
    return pl.pallas_call(
        functools.partial(_ffn_kernel, final_norm=final_norm),
        grid=(T // tm, F // tf),
        in_specs=[
            pl.BlockSpec((tm, D), lambda i, f: (i, 0)),
            pl.BlockSpec((1, D), lambda i, f: (0, 0)),
            pl.BlockSpec((D, tf), lambda i, f: (0, f)),
            pl.BlockSpec((D, tf), lambda i, f: (0, f)),
            pl.BlockSpec((tf, D), lambda i, f: (f, 0)),
            pl.BlockSpec((1, D), lambda i, f: (0, 0)),
        ],
        out_specs=pl.BlockSpec((tm, D), lambda i, f: (i, 0)),
        out_shape=jax.ShapeDtypeStruct((T, D), _F32),
        scratch_shapes=[pltpu.VMEM((tm, D), _BF16)],
        compiler_params=pltpu.CompilerParams(
            dimension_semantics=("parallel", "arbitrary"),
            vmem_limit_bytes=V7X_VMEM_LIMIT_BYTES),
        name="ffn_final" if final_norm else "ffn",
    )(x2d, norm_gain.reshape(1, D), w_gate, w_up, w_down, final_gain.reshape(1, D))


def _lru_scan(a, gx, carry):
    row = lax.broadcasted_iota(jnp.int32, (V7X_SUBLANES, a.shape[1]), 0)
    keeps = [(k, row >= k) for k in (1, 2, 4)]
    out = []
    for g in range(a.shape[0] // V7X_SUBLANES):
        rows = slice(g * V7X_SUBLANES, (g + 1) * V7X_SUBLANES)
        ag, hg = a[rows, :], gx[rows, :]
        for k, keep in keeps:
            a_prev = pltpu.roll(ag, k, axis=0)
            h_prev = pltpu.roll(hg, k, axis=0)
            hg = jnp.where(keep, ag * h_prev + hg, hg)
            ag = jnp.where(keep, ag * a_prev, ag)
        hg = hg + ag * carry
        out.append(hg)
        carry = jnp.broadcast_to(hg[V7X_SUBLANES - 1:V7X_SUBLANES, :], hg.shape)
    return jnp.concatenate(out, axis=0), carry


def _mixer_kernel(x_ref, g_ref, win_ref, vgain_ref, ws_ref, bs_ref, cw_ref, cb_ref,
                  wax_ref, ba_ref, bx_ref, lam_ref, wout_ref, o_ref,
                  y_ref, xpad_ref, hstate_ref):
    s = pl.program_id(1)
    ts = x_ref.shape[0]
    n_win = ts // GMLP_WINDOW
    pad = V7X_SUBLANES

    @pl.when(s == 0)
    def _():
        xpad_ref[ts:ts + pad, :] = jnp.zeros((pad, D_BRANCH), _F32)
        hstate_ref[...] = jnp.zeros(hstate_ref.shape, _F32)

    x = x_ref[...]
    xn = _rms_norm(x, g_ref[...]).astype(_BF16)

    gu = _gelu_tanh(_dot(xn, win_ref[:, 0:D_BRANCH]))
    gv = _gelu_tanh(_dot(xn, win_ref[:, D_BRANCH:2 * D_BRANCH]))
    vn = _rms_norm(gv, vgain_ref[...]).astype(_BF16)
    pos_q = lax.broadcasted_iota(jnp.int32, (GMLP_WINDOW, GMLP_WINDOW), 0)
    pos_p = lax.broadcasted_iota(jnp.int32, (GMLP_WINDOW, GMLP_WINDOW), 1)
    causal = (pos_p // CHUNK) <= (pos_q // CHUNK)
    for h in range(N_HEADS):
        cols = slice(h * HEAD_DIM, (h + 1) * HEAD_DIM)
        w_h = jnp.where(causal, ws_ref[h], 0.0).astype(_BF16)
        rhs = jnp.concatenate(
            [vn[w * GMLP_WINDOW:(w + 1) * GMLP_WINDOW, cols] for w in range(n_win)], axis=1)
        mixed = _dot(w_h, rhs)
        for w in range(n_win):
            rows = slice(w * GMLP_WINDOW, (w + 1) * GMLP_WINDOW)
            m = mixed[:, w * HEAD_DIM:(w + 1) * HEAD_DIM] + bs_ref[:, cols]
            y_ref[rows, cols] = (gu[rows, cols] * m).astype(_BF16)

    xpad_ref[0:pad, :] = xpad_ref[ts:ts + pad, :]
    xpad_ref[pad:pad + ts, :] = _dot(xn, win_ref[:, 3 * D_BRANCH:4 * D_BRANCH])
    xc = cb_ref[...] + cw_ref[CONV_WIDTH - 1:CONV_WIDTH, :] * xpad_ref[pad:pad + ts, :]
    for k in range(CONV_WIDTH - 1):
        off = pad - (CONV_WIDTH - 1) + k
        xc = xc + cw_ref[k:k + 1, :] * xpad_ref[off:off + ts, :]

    gg = _gelu_tanh(_dot(xn, win_ref[:, 2 * D_BRANCH:3 * D_BRANCH]))
    xcb = xc.astype(_BF16)
    neg_lam = -lam_ref[...]
    softplus_neg_lam = jnp.maximum(neg_lam, 0.0) + jnp.log1p(jnp.exp(-jnp.abs(neg_lam)))
    for h in range(N_HEADS):
        cols = slice(h * HEAD_DIM, (h + 1) * HEAD_DIM)
        ri = _dot(xcb[:, cols], wax_ref[h])
        r = jax.nn.sigmoid(ri[:, 0:HEAD_DIM] + ba_ref[:, cols])
        i = jax.nn.sigmoid(ri[:, HEAD_DIM:2 * HEAD_DIM] + bx_ref[:, cols])
        log_a = (-LRU_C) * r * softplus_neg_lam[:, cols]
        a = jnp.exp(log_a)
        one_minus_a2 = -jnp.tanh(log_a) * (a * a + 1.0)
        gated_x = jnp.sqrt(one_minus_a2) * (i * xc[:, cols])
        hs, carry = _lru_scan(a, gated_x, hstate_ref[:, cols])
        hstate_ref[:, cols] = carry
        y_ref[:, D_BRANCH + h * HEAD_DIM:D_BRANCH + (h + 1) * HEAD_DIM] = (
            hs * gg[:, cols]).astype(_BF16)

    o_ref[...] = x + _dot(y_ref[...], wout_ref[...])


def _mixer(x, norm_gain, w_in, v_gain, w_s, b_s_full, conv_w, conv_b, w_ax, b_a, b_x,
           lam, w_out):
    B, S, D = x.shape
    ts = MIXER_SEQ_TILE
    assert S % ts == 0 and ts % GMLP_WINDOW == 0

    def whole(arr):
        nd = arr.ndim
        return pl.BlockSpec(arr.shape, lambda b, s: (0,) * nd, pipeline_mode=pl.Buffered(1))

    operands = (norm_gain.reshape(1, D), w_in, v_gain.reshape(1, D_BRANCH), w_s, b_s_full,
                conv_w, conv_b.reshape(1, D_BRANCH), w_ax, b_a.reshape(1, D_BRANCH),
                b_x.reshape(1, D_BRANCH), lam.reshape(1, D_BRANCH), w_out)
    return pl.pallas_call(
        _mixer_kernel,
        grid=(B, S // ts),
        in_specs=[pl.BlockSpec((None, ts, D), lambda b, s: (b, s, 0))]
                 + [whole(a) for a in operands],
        out_specs=pl.BlockSpec((None, ts, D), lambda b, s: (b, s, 0)),
        out_shape=jax.ShapeDtypeStruct((B, S, D), _F32),
        scratch_shapes=[
            pltpu.VMEM((ts, 2 * D_BRANCH), _BF16),
            pltpu.VMEM((ts + V7X_SUBLANES, D_BRANCH), _F32),
            pltpu.VMEM((V7X_SUBLANES, D_BRANCH), _F32),
        ],
        compiler_params=pltpu.CompilerParams(
            dimension_semantics=("arbitrary", "arbitrary"),
            vmem_limit_bytes=V7X_VMEM_LIMIT_BYTES),
        name="mixer",
    )(x, *operands)


def kernel(x, ffn1_norm, ffn1_w_gate, ffn1_w_up, ffn1_w_down, mix_norm, w_in, gmlp_v_gain, gmlp_w_s, gmlp_b_s, conv_w, conv_b, lru_w_a, lru_b_a, lru_w_x, lru_b_x, lru_lambda, w_out, ffn2_norm, ffn2_w_gate, ffn2_w_up, ffn2_w_down, final_norm):
    B, S, D = x.shape
    depth = ffn1_norm.shape[0]
    bf = lambda w: w.astype(_BF16)
    w_ax = bf(jnp.concatenate([lru_w_a, lru_w_x], axis=-1))
    b_s_full = jnp.repeat(jnp.swapaxes(gmlp_b_s, 1, 2), HEAD_DIM, axis=2)
    for l in range(depth):
        x2d = _ffn(x.reshape(B * S, D), ffn1_norm[l], bf(ffn1_w_gate[l]), bf(ffn1_w_up[l]),
                   bf(ffn1_w_down[l]), final_norm, final_norm=False)
        x = _mixer(x2d.reshape(B, S, D), mix_norm[l], bf(w_in[l]), gmlp_v_gain[l], gmlp_w_s[l],
                   b_s_full[l], conv_w[l], conv_b[l], w_ax[l], lru_b_a[l], lru_b_x[l],
                   lru_lambda[l], bf(w_out[l]))
        x2d = _ffn(x.reshape(B * S, D), ffn2_norm[l], bf(ffn2_w_gate[l]), bf(ffn2_w_up[l]),
                   bf(ffn2_w_down[l]), final_norm, final_norm=(l == depth - 1))
        x = x2d.reshape(B, S, D)
    return x
```

```python
import functools
import math

import jax
import jax.numpy as jnp
from jax import lax
from jax.experimental import pallas as pl
from jax.experimental.pallas import tpu as pltpu

CHUNK = 64
GMLP_WINDOW = 128
HEAD_DIM = 128
N_HEADS = 8
D_BRANCH = N_HEADS * HEAD_DIM
CONV_WIDTH = 4
LRU_C = 8.0
EPS = 1e-6

V7X_SUBLANES = 8
V7X_VMEM_LIMIT_BYTES = 60000 * 1024

FFN_TOKEN_TILE = 1024
FFN_HIDDEN_TILE = 512
MIXER_SEQ_TILE = 256
CAST_BLOCK_BYTES = 8 * 1024 * 1024

_F32 = jnp.float32
_BF16 = jnp.bfloat16


def _rms_norm(x, gain):
    return x * lax.rsqrt(jnp.mean(x * x, axis=-1, keepdims=True) + EPS) * gain


def _gelu_tanh(x):
    c = math.sqrt(2.0 / math.pi)
    return x * (0.5 * (1.0 + jnp.tanh(c * (x + 0.044715 * (x * x * x)))))


def _dot(a, b):
    return jnp.dot(a, b, preferred_element_type=_F32)


def _cast_kernel(w_ref, o_ref):
    o_ref[...] = w_ref[...].astype(o_ref.dtype)


def _to_bf16(w):
    depth, R, C = w.shape
    rows = CAST_BLOCK_BYTES // (C * w.dtype.itemsize)
    rows = 1 << (rows.bit_length() - 1)
    while R % rows:
        rows //= 2
    assert rows >= 2 * V7X_SUBLANES
    spec = pl.BlockSpec((None, rows, C), lambda l, r: (l, r, 0))
    return pl.pallas_call(
        _cast_kernel,
        grid=(depth, R // rows),
        in_specs=[spec],
        out_specs=spec,
        out_shape=jax.ShapeDtypeStruct(w.shape, _BF16),
        compiler_params=pltpu.CompilerParams(
            dimension_semantics=("parallel", "parallel"),
            vmem_limit_bytes=V7X_VMEM_LIMIT_BYTES),
        name="cast_bf16",
    )(w)


def _ffn_kernel(x_ref, g_ref, wg_ref, wu_ref, wd_ref, fg_ref, o_ref, xn_ref, *, final_norm):
    f = pl.program_id(1)
    last = pl.num_programs(1) - 1

    @pl.when(f == 0)
    def _():
        x = x_ref[...]
        xn_ref[...] = _rms_norm(x, g_ref[...]).astype(_BF16)
        o_ref[...] = x

    xn = xn_ref[...]
    gate = _dot(xn, wg_ref[...])
    up = _dot(xn, wu_ref[...])
    h = (0.5 * gate * jax.nn.sigmoid(gate) * up).astype(_BF16)
    o_ref[...] += _dot(h, wd_ref[...])

    if final_norm:
        @pl.when(f == last)
        def _():
            o_ref[...] = _rms_norm(o_ref[...], fg_ref[...])


def _ffn(x2d, norm_gain, w_gate, w_up, w_down, final_gain, *, layer, final_norm):
    T, D = x2d.shape
    F = w_gate.shape[2]
    tm, tf = FFN_TOKEN_TILE, FFN_HIDDEN_TILE
    assert T % tm == 0 and F % tf == 0
    return pl.pallas_call(
        functools.partial(_ffn_kernel, final_norm=final_norm),
        grid=(T // tm, F // tf),
        in_specs=[
            pl.BlockSpec((tm, D), lambda i, f: (i, 0)),
            pl.BlockSpec((1, D), lambda i, f: (0, 0)),
            pl.BlockSpec((None, D, tf), lambda i, f: (layer, 0, f)),
            pl.BlockSpec((None, D, tf), lambda i, f: (layer, 0, f)),
            pl.BlockSpec((None, tf, D), lambda i, f: (layer, f, 0)),
            pl.BlockSpec((1, D), lambda i, f: (0, 0)),
        ],
        out_specs=pl.BlockSpec((tm, D), lambda i, f: (i, 0)),
        out_shape=jax.ShapeDtypeStruct((T, D), _F32),
        scratch_shapes=[pltpu.VMEM((tm, D), _BF16)],
        compiler_params=pltpu.CompilerParams(
            dimension_semantics=("parallel", "arbitrary"),
            vmem_limit_bytes=V7X_VMEM_LIMIT_BYTES),
        name="ffn_final" if final_norm else "ffn",
    )(x2d, norm_gain.reshape(1, D), w_gate, w_up, w_down, final_gain.reshape(1, D))


def _lru_scan(a, gx, carry):
    row = lax.broadcasted_iota(jnp.int32, (V7X_SUBLANES, a.shape[1]), 0)
    keeps = [(k, row >= k) for k in (1, 2, 4)]
    out = []
    for g in range(a.shape[0] // V7X_SUBLANES):
        rows = slice(g * V7X_SUBLANES, (g + 1) * V7X_SUBLANES)
        ag, hg = a[rows, :], gx[rows, :]
        for k, keep in keeps:
            a_prev = pltpu.roll(ag, k, axis=0)
            h_prev = pltpu.roll(hg, k, axis=0)
            hg = jnp.where(keep, ag * h_prev + hg, hg)
            ag = jnp.where(keep, ag * a_prev, ag)
        hg = hg + ag * carry
        out.append(hg)
        carry = jnp.broadcast_to(hg[V7X_SUBLANES - 1:V7X_SUBLANES, :], hg.shape)
    return jnp.concatenate(out, axis=0), carry


def _mixer_kernel(x_ref, g_ref, win_ref, vgain_ref, ws_ref, bs_ref, cw_ref, cb_ref,
                  wax_ref, ba_ref, bx_ref, lam_ref, wout_ref, o_ref,
                  y_ref, xpad_ref, hstate_ref):
    s = pl.program_id(1)
    ts = x_ref.shape[0]
    n_win = ts // GMLP_WINDOW
    pad = V7X_SUBLANES

    @pl.when(s == 0)
    def _():
        xpad_ref[ts:ts + pad, :] = jnp.zeros((pad, D_BRANCH), _F32)
        hstate_ref[...] = jnp.zeros(hstate_ref.shape, _F32)

    x = x_ref[...]
    xn = _rms_norm(x, g_ref[...]).astype(_BF16)

    gu = _gelu_tanh(_dot(xn, win_ref[:, 0:D_BRANCH]))
    gv = _gelu_tanh(_dot(xn, win_ref[:, D_BRANCH:2 * D_BRANCH]))
    vn = _rms_norm(gv, vgain_ref[...]).astype(_BF16)
    pos_q = lax.broadcasted_iota(jnp.int32, (GMLP_WINDOW, GMLP_WINDOW), 0)
    pos_p = lax.broadcasted_iota(jnp.int32, (GMLP_WINDOW, GMLP_WINDOW), 1)
    causal = (pos_p // CHUNK) <= (pos_q // CHUNK)
    for h in range(N_HEADS):
        cols = slice(h * HEAD_DIM, (h + 1) * HEAD_DIM)
        w_h = jnp.where(causal, ws_ref[h], 0.0).astype(_BF16)
        rhs = jnp.concatenate(
            [vn[w * GMLP_WINDOW:(w + 1) * GMLP_WINDOW, cols] for w in range(n_win)], axis=1)
        mixed = _dot(w_h, rhs)
        for w in range(n_win):
            rows = slice(w * GMLP_WINDOW, (w + 1) * GMLP_WINDOW)
            m = mixed[:, w * HEAD_DIM:(w + 1) * HEAD_DIM] + bs_ref[:, cols]
            y_ref[rows, cols] = (gu[rows, cols] * m).astype(_BF16)

    xpad_ref[0:pad, :] = xpad_ref[ts:ts + pad, :]
    xpad_ref[pad:pad + ts, :] = _dot(xn, win_ref[:, 3 * D_BRANCH:4 * D_BRANCH])
    xc = cb_ref[...] + cw_ref[CONV_WIDTH - 1:CONV_WIDTH, :] * xpad_ref[pad:pad + ts, :]
    for k in range(CONV_WIDTH - 1):
        off = pad - (CONV_WIDTH - 1) + k
        xc = xc + cw_ref[k:k + 1, :] * xpad_ref[off:off + ts, :]

    gg = _gelu_tanh(_dot(xn, win_ref[:, 2 * D_BRANCH:3 * D_BRANCH]))
    xcb = xc.astype(_BF16)
    neg_lam = -lam_ref[...]
    softplus_neg_lam = jnp.maximum(neg_lam, 0.0) + jnp.log1p(jnp.exp(-jnp.abs(neg_lam)))
    for h in range(N_HEADS):
        cols = slice(h * HEAD_DIM, (h + 1) * HEAD_DIM)
        ri = _dot(xcb[:, cols], wax_ref[h])
        r = jax.nn.sigmoid(ri[:, 0:HEAD_DIM] + ba_ref[:, cols])
        i = jax.nn.sigmoid(ri[:, HEAD_DIM:2 * HEAD_DIM] + bx_ref[:, cols])
        log_a = (-LRU_C) * r * softplus_neg_lam[:, cols]
        a = jnp.exp(log_a)
        one_minus_a2 = -jnp.tanh(log_a) * (a * a + 1.0)
        gated_x = jnp.sqrt(one_minus_a2) * (i * xc[:, cols])
        hs, carry = _lru_scan(a, gated_x, hstate_ref[:, cols])
        hstate_ref[:, cols] = carry
        y_ref[:, D_BRANCH + h * HEAD_DIM:D_BRANCH + (h + 1) * HEAD_DIM] = (
            hs * gg[:, cols]).astype(_BF16)

    o_ref[...] = x + _dot(y_ref[...], wout_ref[...])


def _mixer(x, norm_gain, w_in, v_gain, w_s, b_s_full, conv_w, conv_b, w_ax, b_a, b_x,
           lam, w_out, *, layer):
    B, S, D = x.shape
    ts = MIXER_SEQ_TILE
    assert S % ts == 0 and ts % GMLP_WINDOW == 0

    def whole(arr):
        nd = arr.ndim
        return pl.BlockSpec(arr.shape, lambda b, s: (0,) * nd, pipeline_mode=pl.Buffered(1))

    def layer_slab(arr):
        return pl.BlockSpec((None,) + arr.shape[1:], lambda b, s: (layer, 0, 0),
                            pipeline_mode=pl.Buffered(1))

    operands = (norm_gain.reshape(1, D), w_in, v_gain.reshape(1, D_BRANCH), w_s, b_s_full,
                conv_w, conv_b.reshape(1, D_BRANCH), w_ax, b_a.reshape(1, D_BRANCH),
                b_x.reshape(1, D_BRANCH), lam.reshape(1, D_BRANCH), w_out)
    param_specs = [layer_slab(a) if a is w_in or a is w_out else whole(a) for a in operands]
    return pl.pallas_call(
        _mixer_kernel,
        grid=(B, S // ts),
        in_specs=[pl.BlockSpec((None, ts, D), lambda b, s: (b, s, 0))] + param_specs,
        out_specs=pl.BlockSpec((None, ts, D), lambda b, s: (b, s, 0)),
        out_shape=jax.ShapeDtypeStruct((B, S, D), _F32),
        scratch_shapes=[
            pltpu.VMEM((ts, 2 * D_BRANCH), _BF16),
            pltpu.VMEM((ts + V7X_SUBLANES, D_BRANCH), _F32),
            pltpu.VMEM((V7X_SUBLANES, D_BRANCH), _F32),
        ],
        compiler_params=pltpu.CompilerParams(
            dimension_semantics=("arbitrary", "arbitrary"),
            vmem_limit_bytes=V7X_VMEM_LIMIT_BYTES),
        name="mixer",
    )(x, *operands)


def kernel(x, ffn1_norm, ffn1_w_gate, ffn1_w_up, ffn1_w_down, mix_norm, w_in, gmlp_v_gain, gmlp_w_s, gmlp_b_s, conv_w, conv_b, lru_w_a, lru_b_a, lru_w_x, lru_b_x, lru_lambda, w_out, ffn2_norm, ffn2_w_gate, ffn2_w_up, ffn2_w_down, final_norm):
    B, S, D = x.shape
    depth = ffn1_norm.shape[0]
    w_ax = jnp.concatenate([lru_w_a, lru_w_x], axis=-1).astype(_BF16)
    b_s_full = jnp.repeat(jnp.swapaxes(gmlp_b_s, 1, 2), HEAD_DIM, axis=2)
    ffn1 = [_to_bf16(w) for w in (ffn1_w_gate, ffn1_w_up, ffn1_w_down)]
    ffn2 = [_to_bf16(w) for w in (ffn2_w_gate, ffn2_w_up, ffn2_w_down)]
    w_in_bf, w_out_bf = _to_bf16(w_in), _to_bf16(w_out)
    for l in range(depth):
        x2d = _ffn(x.reshape(B * S, D), ffn1_norm[l], *ffn1, final_norm,
                   layer=l, final_norm=False)
        x = _mixer(x2d.reshape(B, S, D), mix_norm[l], w_in_bf, gmlp_v_gain[l], gmlp_w_s[l],
                   b_s_full[l], conv_w[l], conv_b[l], w_ax[l], lru_b_a[l], lru_b_x[l],
                   lru_lambda[l], w_out_bf, layer=l)
        x2d = _ffn(x.reshape(B * S, D), ffn2_norm[l], *ffn2, final_norm,
                   layer=l, final_norm=(l == depth - 1))
        x = x2d.reshape(B, S, D)
    return x
```
